```python
import jax, jax.numpy as jnp
from jax import lax
import numpy as np

D_MODEL = 1024
BATCH = 8
SEQ = 4096
DEPTH = 1

CHUNK = 64
GLA_HEADS = 4
GLA_DK = D_MODEL // 2 // GLA_HEADS
GLA_DV = D_MODEL // GLA_HEADS
GLA_QK = GLA_HEADS * GLA_DK
GLA_V = GLA_HEADS * GLA_DV
GLA_GATE_RANK = 16
GLA_TAU = 16.0
SGU_GROUPS = 4
SGU_BLOCK = 128
SGU_WIDTH = D_MODEL
SGU_DG = SGU_WIDTH // SGU_GROUPS
D_FF = -(-8 * D_MODEL // (3 * 256)) * 256
EPS = 1e-6

IN_SPLITS = (GLA_QK, GLA_QK, GLA_V, GLA_V, GLA_GATE_RANK,
             SGU_WIDTH, SGU_WIDTH, D_MODEL, D_MODEL)
D_IN = int(sum(IN_SPLITS))
IN_OFFSETS = tuple(int(o) for o in np.cumsum(IN_SPLITS)[:-1])

kernel_name = "hybrid_gla_sgu_swiglu_sandwich"


def rmsnorm(x, g):
    xf = x.astype(jnp.float32)
    y = xf * lax.rsqrt(jnp.mean(xf * xf, axis=-1, keepdims=True) + EPS)
    return (y * g.astype(jnp.float32)).astype(x.dtype)


def gla_branch(q, k, v, r, a_low, w_gate_up, b_gate, gla_norm):
    B, S, _ = q.shape
    N = S // CHUNK
    f32 = jnp.float32

    def heads(t, d):
        return t.reshape(B, N, CHUNK, GLA_HEADS, d)

    qh = heads(q, GLA_DK).astype(f32) * (GLA_DK ** -0.5)
    kh = heads(k, GLA_DK).astype(f32)
    vh = heads(v, GLA_DV).astype(f32)
    logit = jnp.einsum('bsr,rk->bsk', a_low, w_gate_up) + b_gate
    log_a = heads(jax.nn.log_sigmoid(logit.astype(f32)) / GLA_TAU, GLA_DK)
    cum = jnp.cumsum(log_a, axis=2)
    tot = cum[:, :, -1]
    k_dec = kh * jnp.exp(tot[:, :, None] - cum)
    upd = jnp.einsum('bnchk,bnchv->bnhkv', k_dec, vh)
    decay = jnp.exp(tot)

    def step(state, inp):
        a_c, u_c = inp
        state = a_c[..., None] * state + u_c
        return state, state

    s0 = jnp.zeros((B, GLA_HEADS, GLA_DK, GLA_DV), f32)
    _, states = lax.scan(step, s0, (jnp.moveaxis(decay, 1, 0), jnp.moveaxis(upd, 1, 0)))
    states = jnp.moveaxis(states, 0, 1)
    o = jnp.einsum('bnchk,bnhkv->bnchv', qh, states)
    o = o * lax.rsqrt(jnp.mean(o * o, axis=-1, keepdims=True) + EPS) * gla_norm.astype(f32)
    o = o.reshape(B, S, GLA_V).astype(q.dtype)
    return o * jax.nn.silu(r)


def sgu_branch(u, v, ln_g, ln_b, w_spatial, b_spatial):
    B, S, _ = u.shape
    u = jax.nn.gelu(u)
    vf = jax.nn.gelu(v).reshape(B, S, SGU_GROUPS, SGU_DG).astype(jnp.float32)
    mu = jnp.mean(vf, axis=-1, keepdims=True)
    var = jnp.mean(jnp.square(vf - mu), axis=-1, keepdims=True)
    vn = (vf - mu) * lax.rsqrt(var + EPS) * ln_g.astype(jnp.float32) + ln_b.astype(jnp.float32)
    vn = vn.astype(u.dtype).reshape(B, S // SGU_BLOCK, SGU_BLOCK, SGU_GROUPS, SGU_DG)
    pos = np.arange(SGU_BLOCK)
    mask = (pos[None, :] // CHUNK) <= (pos[:, None] // CHUNK)
    w = jnp.where(mask[None], w_spatial, 0.0)
    mixed = jnp.einsum('gij,bnjgc->bnigc', w, vn) + b_spatial.T[:, :, None]
    return u * mixed.reshape(B, S, SGU_WIDTH)


def setup_inputs(seed: int = 0) -> dict:
    key = jax.random.key(seed)
    ks = jax.random.split(key, 20)
    L = DEPTH
    nrm = jax.random.normal

    def gain(k, shape):
        return 1.0 + 0.05 * nrm(k, shape, jnp.float32)

    return {
        "x": nrm(ks[0], (BATCH, SEQ, D_MODEL), jnp.float32),
        "norm_pre_mix": gain(ks[1], (L, D_MODEL)),
        "w_in": nrm(ks[2], (L, D_MODEL, D_IN), jnp.float32) * D_MODEL ** -0.5,
        "w_gate_up": nrm(ks[3], (L, GLA_GATE_RANK, GLA_QK), jnp.float32) * GLA_GATE_RANK ** -0.5,
        "b_gate": 0.01 * nrm(ks[4], (L, GLA_QK), jnp.float32),
        "gla_norm": gain(ks[5], (L, GLA_HEADS, GLA_DV)),
        "sgu_ln_g": gain(ks[6], (L, SGU_GROUPS, SGU_DG)),
        "sgu_ln_b": 0.02 * nrm(ks[7], (L, SGU_GROUPS, SGU_DG), jnp.float32),
        "w_spatial": nrm(ks[8], (L, SGU_GROUPS, SGU_BLOCK, SGU_BLOCK), jnp.float32) * SGU_BLOCK ** -0.5,
        "b_spatial": 1.0 + 0.02 * nrm(ks[9], (L, SGU_GROUPS, SGU_BLOCK), jnp.float32),
        "w_branch_gla": nrm(ks[10], (L, GLA_V, D_MODEL), jnp.float32) * GLA_V ** -0.5,
        "w_branch_sgu": nrm(ks[11], (L, SGU_WIDTH, D_MODEL), jnp.float32) * SGU_WIDTH ** -0.5,
        "w_out": nrm(ks[12], (L, D_MODEL, D_MODEL), jnp.float32) * D_MODEL ** -0.5,
        "norm_post_mix": gain(ks[13], (L, D_MODEL)),
        "norm_pre_ffn": gain(ks[14], (L, D_MODEL)),
        "w_ffn_in": nrm(ks[15], (L, D_MODEL, 2 * D_FF), jnp.float32) * D_MODEL ** -0.5,
        "w_ffn_out": nrm(ks[16], (L, D_FF, D_MODEL), jnp.float32) * D_FF ** -0.5,
        "norm_post_ffn": gain(ks[17], (L, D_MODEL)),
    }


def reference(x, norm_pre_mix, w_in, w_gate_up, b_gate, gla_norm, sgu_ln_g, sgu_ln_b,
              w_spatial, b_spatial, w_branch_gla, w_branch_sgu, w_out, norm_post_mix,
              norm_pre_ffn, w_ffn_in, w_ffn_out, norm_post_ffn):
    for l in range(DEPTH):
        a = rmsnorm(x, norm_pre_mix[l])
        proj = jnp.einsum('bsd,de->bse', a, w_in[l])
        q, k, v, r, a_low, su, sv, g_gla, g_sgu = jnp.split(proj, IN_OFFSETS, axis=-1)
        y_gla = gla_branch(q, k, v, r, a_low, w_gate_up[l], b_gate[l], gla_norm[l])
        y_sgu = sgu_branch(su, sv, sgu_ln_g[l], sgu_ln_b[l], w_spatial[l], b_spatial[l])
        merged = (jax.nn.sigmoid(g_gla) * jnp.einsum('bsv,vd->bsd', y_gla, w_branch_gla[l])
                  + jax.nn.sigmoid(g_sgu) * jnp.einsum('bsv,vd->bsd', y_sgu, w_branch_sgu[l]))
        mix = jnp.einsum('bsd,de->bse', merged, w_out[l])
        x = x + rmsnorm(mix, norm_post_mix[l])
        h = rmsnorm(x, norm_pre_ffn[l])
        gate, up = jnp.split(jnp.einsum('bsd,df->bsf', h, w_ffn_in[l]), 2, axis=-1)
        y = jnp.einsum('bsf,fd->bsd', jax.nn.silu(gate) * up, w_ffn_out[l])
        x = x + rmsnorm(y, norm_post_ffn[l])
    return x
```

```python
import functools

import jax
import jax.numpy as jnp
from jax import lax
from jax.experimental import pallas as pl
from jax.experimental.pallas import tpu as pltpu

F32 = jnp.float32
BF16 = jnp.bfloat16

D_MODEL = 1024
CHUNK = 64
GLA_HEADS = 4
GLA_DK = 128
GLA_DV = 256
GLA_QK = GLA_HEADS * GLA_DK
GLA_V = GLA_HEADS * GLA_DV
GLA_GATE_RANK = 16
GLA_TAU = 16.0
SGU_GROUPS = 4
SGU_BLOCK = 128
SGU_DG = 256
D_FF = 2816
EPS = 1e-6

COL_Q = 0
COL_K = COL_Q + GLA_QK
COL_V = COL_K + GLA_QK
COL_R = COL_V + GLA_V
COL_SU = COL_R + GLA_V
COL_SV = COL_SU + D_MODEL
COL_GG = COL_SV + D_MODEL
COL_GS = COL_GG + D_MODEL
N_MAIN = COL_GS + D_MODEL
LANES = 128

TOKEN_TILE = 256
FFN_COL_TILE = 256
VMEM_LIMIT_BYTES = 56 * 1024 * 1024


def _dot(a, b):
    return jnp.dot(a, b, preferred_element_type=F32)


def _rmsnorm(x, g):
    ms = jnp.mean(x * x, axis=-1, keepdims=True)
    return x * lax.rsqrt(ms + EPS) * g


def _sigmoid(x):
    return 1.0 / (1.0 + jnp.exp(-x))


def _gelu_tanh(x):
    c = 0.7978845608028654
    inner = x * (c + (c * 0.044715) * (x * x))
    hx = 0.5 * x
    return hx + hx * jnp.tanh(inner)


def _chunk_cumsum(x):
    row_in_chunk = lax.broadcasted_iota(jnp.int32, x.shape, 0) & (CHUNK - 1)
    shift = 1
    while shift < CHUNK:
        shifted = pltpu.roll(x, shift, axis=0)
        x = x + jnp.where(row_in_chunk >= shift, shifted, 0.0)
        shift *= 2
    return x


def _mixer_kernel(x_ref, gpre_ref, w_ref, walow_ref, wgu_ref, bgate_ref, gnorm_ref,
                  lng_ref, lnb_ref, wsp_ref, bsp_ref, wbg_ref, wbs_ref, wout_ref,
                  gpost_ref, o_ref, state_ref, oacc_ref):
    tm = x_ref.shape[1]

    @pl.when(pl.program_id(1) == 0)
    def _():
        state_ref[...] = jnp.zeros_like(state_ref)

    x = x_ref[0]
    a = _rmsnorm(x, gpre_ref[...]).astype(BF16)

    def proj(c0, width):
        return _dot(a, w_ref[:, c0:c0 + width])

    q = (proj(COL_Q, GLA_QK) * (GLA_DK ** -0.5)).astype(BF16)
    k = proj(COL_K, GLA_QK)
    v = proj(COL_V, GLA_V).astype(BF16)
    a_low = _dot(a, walow_ref[...]).astype(BF16)
    logit = _dot(a_low, wgu_ref[...]) + bgate_ref[...]
    log_sig = jnp.minimum(logit, 0.0) - jnp.log1p(jnp.exp(-jnp.abs(logit)))
    cum = _chunk_cumsum(log_sig * (1.0 / GLA_TAU))

    for c in range(tm // CHUNK):
        r0 = c * CHUNK
        cum_c = cum[r0:r0 + CHUNK]
        tot_c = cum[r0 + CHUNK - 1:r0 + CHUNK]
        k_dec = (k[r0:r0 + CHUNK] * jnp.exp(tot_c - cum_c)).astype(BF16)
        decay = jnp.exp(tot_c)
        for h in range(GLA_HEADS):
            kd = k_dec[:, h * GLA_DK:(h + 1) * GLA_DK]
            vv = v[r0:r0 + CHUNK, h * GLA_DV:(h + 1) * GLA_DV]
            upd = lax.dot_general(kd, vv, (((0,), (0,)), ((), ())),
                                  preferred_element_type=F32)
            dcol = jnp.broadcast_to(decay[:, h * GLA_DK:(h + 1) * GLA_DK], (GLA_DK, GLA_DK)).T
            s_new = jnp.concatenate([dcol, dcol], axis=1) * state_ref[h] + upd
            state_ref[h] = s_new
            oacc_ref[r0:r0 + CHUNK, h * GLA_DV:(h + 1) * GLA_DV] = _dot(
                q[r0:r0 + CHUNK, h * GLA_DK:(h + 1) * GLA_DK], s_new.astype(BF16))

    r = proj(COL_R, GLA_V)
    gnorm = gnorm_ref[...]
    o_heads = []
    for h in range(GLA_HEADS):
        sl = slice(h * GLA_DV, (h + 1) * GLA_DV)
        o_heads.append(_rmsnorm(oacc_ref[:, sl], gnorm[:, sl]))
    y_gla = (jnp.concatenate(o_heads, axis=1) * (r * _sigmoid(r))).astype(BF16)

    u = _gelu_tanh(proj(COL_SU, D_MODEL))
    gv = _gelu_tanh(proj(COL_SV, D_MODEL))
    lng = lng_ref[...]
    lnb = lnb_ref[...]
    vn_groups = []
    for g in range(SGU_GROUPS):
        sl = slice(g * SGU_DG, (g + 1) * SGU_DG)
        vg = gv[:, sl]
        dlt = vg - jnp.mean(vg, axis=-1, keepdims=True)
        var = jnp.mean(dlt * dlt, axis=-1, keepdims=True)
        vn_groups.append((dlt * lax.rsqrt(var + EPS) * lng[:, sl] + lnb[:, sl]).astype(BF16))
    row = lax.broadcasted_iota(jnp.int32, (SGU_BLOCK, SGU_BLOCK), 0)
    col = lax.broadcasted_iota(jnp.int32, (SGU_BLOCK, SGU_BLOCK), 1)
    causal = (col // CHUNK) <= (row // CHUNK)
    mixed_blocks = []
    for n in range(tm // SGU_BLOCK):
        rs = slice(n * SGU_BLOCK, (n + 1) * SGU_BLOCK)
        mixed_groups = []
        for g in range(SGU_GROUPS):
            w_g = jnp.where(causal, wsp_ref[g], 0.0).astype(BF16)
            mixed_groups.append(_dot(w_g, vn_groups[g][rs]))
        mixed_blocks.append(jnp.concatenate(mixed_groups, axis=1) + bsp_ref[...])
    y_sgu = (u * jnp.concatenate(mixed_blocks, axis=0)).astype(BF16)

    merged = (_sigmoid(proj(COL_GG, D_MODEL)) * _dot(y_gla, wbg_ref[...])
              + _sigmoid(proj(COL_GS, D_MODEL)) * _dot(y_sgu, wbs_ref[...]))
    mix = _dot(merged.astype(BF16), wout_ref[...])
    o_ref[0] = x + _rmsnorm(mix, gpost_ref[...])


def _ffn_kernel(x_ref, gpre_ref, win_ref, wout_ref, gpost_ref, o_ref, act_ref):
    x = x_ref[...]
    h = _rmsnorm(x, gpre_ref[...]).astype(BF16)
    for j in range(D_FF // FFN_COL_TILE):
        c0 = j * FFN_COL_TILE
        gate = _dot(h, win_ref[:, c0:c0 + FFN_COL_TILE])
        up = _dot(h, win_ref[:, D_FF + c0:D_FF + c0 + FFN_COL_TILE])
        act_ref[:, c0:c0 + FFN_COL_TILE] = (gate * _sigmoid(gate) * up).astype(BF16)
    y = _dot(act_ref[...], wout_ref[...])
    o_ref[...] = x + _rmsnorm(y, gpost_ref[...])


def _resident(shape):
    zeros = (0,) * len(shape)
    return pl.BlockSpec(shape, lambda *_: zeros, pipeline_mode=pl.Buffered(1))


def _mixer_call(x, gpre, w_main, w_alow, w_gu, b_gate, gnorm, lng, lnb, wsp, bsp,
                wbg, wbs, wout, gpost):
    batch, seq, _ = x.shape
    tm = TOKEN_TILE
    consts = (gpre, w_main, w_alow, w_gu, b_gate, gnorm, lng, lnb, wsp, bsp,
              wbg, wbs, wout, gpost)
    return pl.pallas_call(
        _mixer_kernel,
        grid=(batch, seq // tm),
        in_specs=[pl.BlockSpec((1, tm, D_MODEL), lambda b, i: (b, i, 0))]
        + [_resident(c.shape) for c in consts],
        out_specs=pl.BlockSpec((1, tm, D_MODEL), lambda b, i: (b, i, 0)),
        out_shape=jax.ShapeDtypeStruct(x.shape, x.dtype),
        scratch_shapes=[
            pltpu.VMEM((GLA_HEADS, GLA_DK, GLA_DV), F32),
            pltpu.VMEM((tm, GLA_V), F32),
        ],
        compiler_params=pltpu.CompilerParams(
            dimension_semantics=("arbitrary", "arbitrary"),
            vmem_limit_bytes=VMEM_LIMIT_BYTES),
        name="mixer",
    )(x, *consts)


def _ffn_call(x2d, gpre, win, wout, gpost):
    tokens = x2d.shape[0]
    tm = TOKEN_TILE
    consts = (gpre, win, wout, gpost)
    return pl.pallas_call(
        _ffn_kernel,
        grid=(tokens // tm,),
        in_specs=[pl.BlockSpec((tm, D_MODEL), lambda i: (i, 0))]
        + [_resident(c.shape) for c in consts],
        out_specs=pl.BlockSpec((tm, D_MODEL), lambda i: (i, 0)),
        out_shape=jax.ShapeDtypeStruct(x2d.shape, x2d.dtype),
        scratch_shapes=[pltpu.VMEM((tm, D_FF), BF16)],
        compiler_params=pltpu.CompilerParams(
            dimension_semantics=("arbitrary",),
            vmem_limit_bytes=VMEM_LIMIT_BYTES),
        name="ffn",
    )(x2d, *consts)


def kernel(x, norm_pre_mix, w_in, w_gate_up, b_gate, gla_norm, sgu_ln_g, sgu_ln_b,
           w_spatial, b_spatial, w_branch_gla, w_branch_sgu, w_out, norm_post_mix,
           norm_pre_ffn, w_ffn_in, w_ffn_out, norm_post_ffn):
    batch, seq, d = x.shape
    assert d == D_MODEL and seq % TOKEN_TILE == 0
    depth = w_in.shape[0]
    off_alow = GLA_QK + GLA_QK + GLA_V + GLA_V
    for l in range(depth):
        wl = w_in[l]
        w_main = jnp.concatenate(
            [wl[:, :off_alow], wl[:, off_alow + GLA_GATE_RANK:]], axis=1).astype(BF16)
        w_alow = jnp.pad(wl[:, off_alow:off_alow + GLA_GATE_RANK],
                         ((0, 0), (0, LANES - GLA_GATE_RANK))).astype(BF16)
        w_gu = jnp.pad(w_gate_up[l], ((0, LANES - GLA_GATE_RANK), (0, 0))).astype(BF16)
        bsp = jnp.repeat(b_spatial[l].T, SGU_DG, axis=1)
        x = _mixer_call(
            x, norm_pre_mix[l][None], w_main, w_alow, w_gu, b_gate[l][None],
            gla_norm[l].reshape(1, GLA_V), sgu_ln_g[l].reshape(1, D_MODEL),
            sgu_ln_b[l].reshape(1, D_MODEL), w_spatial[l], bsp,
            w_branch_gla[l].astype(BF16), w_branch_sgu[l].astype(BF16),
            w_out[l].astype(BF16), norm_post_mix[l][None])
        x = _ffn_call(
            x.reshape(batch * seq, d), norm_pre_ffn[l][None], w_ffn_in[l].astype(BF16),
            w_ffn_out[l].astype(BF16), norm_post_ffn[l][None]).reshape(batch, seq, d)
    return x
```

```python
import jax
import jax.numpy as jnp
from jax import lax
from jax.experimental import pallas as pl
from jax.experimental.pallas import tpu as pltpu

F32 = jnp.float32
BF16 = jnp.bfloat16

D_MODEL = 1024
CHUNK = 64
GLA_HEADS = 4
GLA_DK = 128
GLA_DV = 256
GLA_QK = GLA_HEADS * GLA_DK
GLA_V = GLA_HEADS * GLA_DV
GLA_GATE_RANK = 16
GLA_TAU = 16.0
SGU_GROUPS = 4
SGU_BLOCK = 128
SGU_DG = 256
D_FF = 2816
EPS = 1e-6

COL_Q = 0
COL_K = COL_Q + GLA_QK
COL_V = COL_K + GLA_QK
COL_R = COL_V + GLA_V
COL_SU = COL_R + GLA_V
COL_SV = COL_SU + D_MODEL
COL_GG = COL_SV + D_MODEL
COL_GS = COL_GG + D_MODEL
N_MAIN = COL_GS + D_MODEL
LANES = 128

TOKEN_TILE = 512
COL_TILE = 256
VMEM_LIMIT_BYTES = 56 * 1024 * 1024


def _dot(a, b):
    return jnp.dot(a, b, preferred_element_type=F32)


def _rmsnorm(x, g):
    ms = jnp.mean(x * x, axis=-1, keepdims=True)
    return x * lax.rsqrt(ms + EPS) * g


def _sigmoid(x):
    return 1.0 / (1.0 + jnp.exp(-x))


def _gelu_tanh(x):
    c = 0.7978845608028654
    inner = x * (c + (c * 0.044715) * (x * x))
    hx = 0.5 * x
    return hx + hx * jnp.tanh(inner)


def _chunk_cumsum(x):
    row_in_chunk = lax.broadcasted_iota(jnp.int32, x.shape, 0) & (CHUNK - 1)
    shift = 1
    while shift < CHUNK:
        shifted = pltpu.roll(x, shift, axis=0)
        x = x + jnp.where(row_in_chunk >= shift, shifted, 0.0)
        shift *= 2
    return x


def _mixer_kernel(x_ref, gpre_ref, w_ref, walow_ref, wgu_ref, bgate_ref, gnorm_ref,
                  lng_ref, lnb_ref, wsp_ref, bsp_ref, wbg_ref, wbs_ref, wout_ref,
                  gpost_ref, o_ref, state_ref, a_ref, q_ref, k_ref, v_ref, rs_ref, u_ref,
                  vn_ref, sg_ref, oacc_ref, yg_ref, ys_ref, m_ref):
    tm = x_ref.shape[1]
    n_chunks = tm // CHUNK
    ct = COL_TILE

    @pl.when(pl.program_id(1) == 0)
    def _():
        state_ref[...] = jnp.zeros_like(state_ref)

    a_ref[...] = _rmsnorm(x_ref[0], gpre_ref[...]).astype(BF16)

    def proj(c0):
        return _dot(a_ref[...], w_ref[:, c0:c0 + ct])

    a_low = _dot(a_ref[...], walow_ref[...]).astype(BF16)
    logit = _dot(a_low, wgu_ref[...]) + bgate_ref[...]
    for j in range(GLA_QK // ct):
        k_ref[:, j * ct:(j + 1) * ct] = proj(COL_K + j * ct)
    log_sig = jnp.minimum(logit, 0.0) - jnp.log1p(jnp.exp(-jnp.abs(logit)))
    for j in range(GLA_QK // ct):
        q_ref[:, j * ct:(j + 1) * ct] = (proj(COL_Q + j * ct) * (GLA_DK ** -0.5)).astype(BF16)
    cum = _chunk_cumsum(log_sig * (1.0 / GLA_TAU))
    for j in range(GLA_V // ct):
        v_ref[:, j * ct:(j + 1) * ct] = proj(COL_V + j * ct).astype(BF16)

    upd = [[None] * GLA_HEADS for _ in range(n_chunks)]
    dcol = [[None] * GLA_HEADS for _ in range(n_chunks)]
    for c in range(n_chunks):
        r0 = c * CHUNK
        cum_c = cum[r0:r0 + CHUNK]
        tot_c = cum[r0 + CHUNK - 1:r0 + CHUNK]
        k_dec = (k_ref[r0:r0 + CHUNK, :] * jnp.exp(tot_c - cum_c)).astype(BF16)
        decay = jnp.exp(tot_c)
        for h in range(GLA_HEADS):
            kd = k_dec[:, h * GLA_DK:(h + 1) * GLA_DK]
            vv = v_ref[r0:r0 + CHUNK, h * GLA_DV:(h + 1) * GLA_DV]
            upd[c][h] = lax.dot_general(kd, vv, (((0,), (0,)), ((), ())),
                                        preferred_element_type=F32)
            dcol[c][h] = jnp.broadcast_to(
                decay[:, h * GLA_DK:(h + 1) * GLA_DK], (GLA_DK, GLA_DK)).T

    for j in range(GLA_V // ct):
        r = proj(COL_R + j * ct)
        rs_ref[:, j * ct:(j + 1) * ct] = r * _sigmoid(r)

    s_bf = [[None] * GLA_HEADS for _ in range(n_chunks)]
    for h in range(GLA_HEADS):
        s = state_ref[h]
        for c in range(n_chunks):
            s = jnp.concatenate([dcol[c][h], dcol[c][h]], axis=1) * s + upd[c][h]
            s_bf[c][h] = s.astype(BF16)
        state_ref[h] = s

    for j in range(D_MODEL // ct):
        u_ref[:, j * ct:(j + 1) * ct] = _gelu_tanh(proj(COL_SU + j * ct))

    for c in range(n_chunks):
        r0 = c * CHUNK
        for h in range(GLA_HEADS):
            oacc_ref[r0:r0 + CHUNK, h * GLA_DV:(h + 1) * GLA_DV] = _dot(
                q_ref[r0:r0 + CHUNK, h * GLA_DK:(h + 1) * GLA_DK], s_bf[c][h])

    assert ct == SGU_DG
    for g in range(SGU_GROUPS):
        sl = slice(g * SGU_DG, (g + 1) * SGU_DG)
        vg = _gelu_tanh(proj(COL_SV + g * SGU_DG))
        dlt = vg - jnp.mean(vg, axis=-1, keepdims=True)
        var = jnp.mean(dlt * dlt, axis=-1, keepdims=True)
        vn_ref[:, sl] = (dlt * lax.rsqrt(var + EPS) * lng_ref[:, sl] + lnb_ref[:, sl]).astype(BF16)

    assert ct == GLA_DV
    for h in range(GLA_HEADS):
        sl = slice(h * GLA_DV, (h + 1) * GLA_DV)
        yg_ref[:, sl] = (_rmsnorm(oacc_ref[:, sl], gnorm_ref[:, sl]) * rs_ref[:, sl]).astype(BF16)

    for j in range(2 * D_MODEL // ct):
        sg_ref[:, j * ct:(j + 1) * ct] = _sigmoid(proj(COL_GG + j * ct))

    row = lax.broadcasted_iota(jnp.int32, (SGU_BLOCK, SGU_BLOCK), 0)
    col = lax.broadcasted_iota(jnp.int32, (SGU_BLOCK, SGU_BLOCK), 1)
    causal = (col // CHUNK) <= (row // CHUNK)
    for g in range(SGU_GROUPS):
        sl = slice(g * SGU_DG, (g + 1) * SGU_DG)
        w_g = jnp.where(causal, wsp_ref[g], 0.0).astype(BF16)
        for n in range(tm // SGU_BLOCK):
            rs = slice(n * SGU_BLOCK, (n + 1) * SGU_BLOCK)
            mixed = _dot(w_g, vn_ref[rs, sl]) + bsp_ref[:, sl]
            ys_ref[rs, sl] = (u_ref[rs, sl] * mixed).astype(BF16)

    for j in range(D_MODEL // ct):
        sl = slice(j * ct, (j + 1) * ct)
        bg = _dot(yg_ref[...], wbg_ref[:, sl])
        bs = _dot(ys_ref[...], wbs_ref[:, sl])
        m_ref[:, sl] = (sg_ref[:, sl] * bg
                        + sg_ref[:, D_MODEL + j * ct:D_MODEL + (j + 1) * ct] * bs).astype(BF16)

    ssq = jnp.zeros((tm, 1), F32)
    for j in range(D_MODEL // ct):
        sl = slice(j * ct, (j + 1) * ct)
        mix = _dot(m_ref[...], wout_ref[:, sl])
        ssq = ssq + jnp.sum(mix * mix, axis=-1, keepdims=True)
        o_ref[0, :, sl] = mix
    scale = lax.rsqrt(ssq * (1.0 / D_MODEL) + EPS)
    o_ref[0] = x_ref[0] + o_ref[0] * scale * gpost_ref[...]


def _ffn_kernel(x_ref, gpre_ref, win_ref, wout_ref, gpost_ref, o_ref, act_ref):
    ct = COL_TILE
    x = x_ref[...]
    h = _rmsnorm(x, gpre_ref[...]).astype(BF16)
    for j in range(D_FF // ct):
        c0 = j * ct
        gate = _dot(h, win_ref[:, c0:c0 + ct])
        up = _dot(h, win_ref[:, D_FF + c0:D_FF + c0 + ct])
        act_ref[:, c0:c0 + ct] = (gate * _sigmoid(gate) * up).astype(BF16)
    y = _dot(act_ref[...], wout_ref[...])
    o_ref[...] = x + _rmsnorm(y, gpost_ref[...])


def _resident(shape):
    zeros = (0,) * len(shape)
    return pl.BlockSpec(shape, lambda *_: zeros, pipeline_mode=pl.Buffered(1))


def _mixer_call(x, gpre, w_main, w_alow, w_gu, b_gate, gnorm, lng, lnb, wsp, bsp,
                wbg, wbs, wout, gpost):
    batch, seq, _ = x.shape
    tm = TOKEN_TILE
    consts = (gpre, w_main, w_alow, w_gu, b_gate, gnorm, lng, lnb, wsp, bsp,
              wbg, wbs, wout, gpost)
    return pl.pallas_call(
        _mixer_kernel,
        grid=(batch, seq // tm),
        in_specs=[pl.BlockSpec((1, tm, D_MODEL), lambda b, i: (b, i, 0))]
        + [_resident(c.shape) for c in consts],
        out_specs=pl.BlockSpec((1, tm, D_MODEL), lambda b, i: (b, i, 0)),
        out_shape=jax.ShapeDtypeStruct(x.shape, x.dtype),
        scratch_shapes=[
            pltpu.VMEM((GLA_HEADS, GLA_DK, GLA_DV), F32),
            pltpu.VMEM((tm, D_MODEL), BF16),
            pltpu.VMEM((tm, GLA_QK), BF16),
            pltpu.VMEM((tm, GLA_QK), F32),
            pltpu.VMEM((tm, GLA_V), BF16),
            pltpu.VMEM((tm, GLA_V), F32),
            pltpu.VMEM((tm, D_MODEL), F32),
            pltpu.VMEM((tm, D_MODEL), BF16),
            pltpu.VMEM((tm, 2 * D_MODEL), F32),
            pltpu.VMEM((tm, GLA_V), F32),
            pltpu.VMEM((tm, GLA_V), BF16),
            pltpu.VMEM((tm, D_MODEL), BF16),
            pltpu.VMEM((tm, D_MODEL), BF16),
        ],
        compiler_params=pltpu.CompilerParams(
            dimension_semantics=("arbitrary", "arbitrary"),
            vmem_limit_bytes=VMEM_LIMIT_BYTES),
        name="mixer",
    )(x, *consts)


def _ffn_call(x2d, gpre, win, wout, gpost):
    tokens = x2d.shape[0]
    tm = TOKEN_TILE
    consts = (gpre, win, wout, gpost)
    return pl.pallas_call(
        _ffn_kernel,
        grid=(tokens // tm,),
        in_specs=[pl.BlockSpec((tm, D_MODEL), lambda i: (i, 0))]
        + [_resident(c.shape) for c in consts],
        out_specs=pl.BlockSpec((tm, D_MODEL), lambda i: (i, 0)),
        out_shape=jax.ShapeDtypeStruct(x2d.shape, x2d.dtype),
        scratch_shapes=[pltpu.VMEM((tm, D_FF), BF16)],
        compiler_params=pltpu.CompilerParams(
            dimension_semantics=("arbitrary",),
            vmem_limit_bytes=VMEM_LIMIT_BYTES),
        name="ffn",
    )(x2d, *consts)


def kernel(x, norm_pre_mix, w_in, w_gate_up, b_gate, gla_norm, sgu_ln_g, sgu_ln_b,
           w_spatial, b_spatial, w_branch_gla, w_branch_sgu, w_out, norm_post_mix,
           norm_pre_ffn, w_ffn_in, w_ffn_out, norm_post_ffn):
    batch, seq, d = x.shape
    assert d == D_MODEL and seq % TOKEN_TILE == 0
    depth = w_in.shape[0]
    off_alow = GLA_QK + GLA_QK + GLA_V + GLA_V
    for l in range(depth):
        wl = w_in[l]
        w_main = jnp.concatenate(
            [wl[:, :off_alow], wl[:, off_alow + GLA_GATE_RANK:]], axis=1).astype(BF16)
        w_alow = jnp.pad(wl[:, off_alow:off_alow + GLA_GATE_RANK],
                         ((0, 0), (0, LANES - GLA_GATE_RANK))).astype(BF16)
        w_gu = jnp.pad(w_gate_up[l], ((0, LANES - GLA_GATE_RANK), (0, 0))).astype(BF16)
        bsp = jnp.repeat(b_spatial[l].T, SGU_DG, axis=1)
        x = _mixer_call(
            x, norm_pre_mix[l][None], w_main, w_alow, w_gu, b_gate[l][None],
            gla_norm[l].reshape(1, GLA_V), sgu_ln_g[l].reshape(1, D_MODEL),
            sgu_ln_b[l].reshape(1, D_MODEL), w_spatial[l], bsp,
            w_branch_gla[l].astype(BF16), w_branch_sgu[l].astype(BF16),
            w_out[l].astype(BF16), norm_post_mix[l][None])
        x = _ffn_call(
            x.reshape(batch * seq, d), norm_pre_ffn[l][None], w_ffn_in[l].astype(BF16),
            w_ffn_out[l].astype(BF16), norm_post_ffn[l][None]).reshape(batch, seq, d)
    return x
```

```python
import jax
import jax.numpy as jnp
from jax import lax
from jax.experimental import pallas as pl
from jax.experimental.pallas import tpu as pltpu

F32 = jnp.float32
BF16 = jnp.bfloat16

D_MODEL = 1024
CHUNK = 64
GLA_HEADS = 4
GLA_DK = 128
GLA_DV = 256
GLA_QK = GLA_HEADS * GLA_DK
GLA_V = GLA_HEADS * GLA_DV
GLA_GATE_RANK = 16
GLA_TAU = 16.0
SGU_GROUPS = 4
SGU_BLOCK = 128
SGU_DG = 256
D_FF = 2816
EPS = 1e-6

LANES = 128

COL_Q = 0
COL_K = COL_Q + GLA_QK
COL_V = COL_K + GLA_QK
COL_R = COL_V + GLA_V
COL_ALOW = COL_R + GLA_V
WIN_A = COL_ALOW + LANES
COL_SU = 0
COL_SV = COL_SU + D_MODEL
COL_GG = COL_SV + D_MODEL
COL_GS = COL_GG + D_MODEL
WIN_B = COL_GS + D_MODEL

TOKEN_TILE = 512
COL_TILE = 256
VMEM_LIMIT_BYTES = 56 * 1024 * 1024


def _dot(a, b):
    return jnp.dot(a, b, preferred_element_type=F32)


def _rmsnorm(x, g):
    ms = jnp.mean(x * x, axis=-1, keepdims=True)
    return x * lax.rsqrt(ms + EPS) * g


def _sigmoid(x):
    return 1.0 / (1.0 + jnp.exp(-x))


def _gelu_tanh(x):
    c = 0.7978845608028654
    inner = x * (c + (c * 0.044715) * (x * x))
    hx = 0.5 * x
    return hx + hx * jnp.tanh(inner)


def _chunk_cumsum(x):
    row_in_chunk = lax.broadcasted_iota(jnp.int32, x.shape, 0) & (CHUNK - 1)
    shift = 1
    while shift < CHUNK:
        shifted = pltpu.roll(x, shift, axis=0)
        x = x + jnp.where(row_in_chunk >= shift, shifted, 0.0)
        shift *= 2
    return x


def _mixer_kernel(x_ref, gpre_ref, wa_ref, wb_ref, wgu_ref, bgate_ref, gnorm_ref,
                  lng_ref, lnb_ref, wsp_ref, bsp_ref, wbg_ref, wbs_ref, wout_ref,
                  gpost_ref, o_ref, state_ref, a_ref, q_ref, k_ref, v_ref, rs_ref, u_ref,
                  vn_ref, sg_ref, oacc_ref, yg_ref, ys_ref, m_ref):
    tm = x_ref.shape[1]
    n_chunks = tm // CHUNK
    ct = COL_TILE

    @pl.when(pl.program_id(1) == 0)
    def _():
        state_ref[...] = jnp.zeros_like(state_ref)

    a_ref[...] = _rmsnorm(x_ref[0], gpre_ref[...]).astype(BF16)

    def proj_a(c0):
        return _dot(a_ref[...], wa_ref[:, c0:c0 + ct])

    def proj_b(c0):
        return _dot(a_ref[...], wb_ref[:, c0:c0 + ct])

    def cols(j):
        return slice(j * ct, (j + 1) * ct)

    def tile_k(j):
        k_ref[:, cols(j)] = proj_a(COL_K + j * ct)

    def tile_q(j):
        q_ref[:, cols(j)] = (proj_a(COL_Q + j * ct) * (GLA_DK ** -0.5)).astype(BF16)

    def tile_v(j):
        v_ref[:, cols(j)] = proj_a(COL_V + j * ct).astype(BF16)

    def tile_r(j):
        r = proj_a(COL_R + j * ct)
        rs_ref[:, cols(j)] = r * _sigmoid(r)

    def tile_su(j):
        u_ref[:, cols(j)] = _gelu_tanh(proj_b(COL_SU + j * ct))

    def tile_sv(g):
        vg = _gelu_tanh(proj_b(COL_SV + g * ct))
        dlt = vg - jnp.mean(vg, axis=-1, keepdims=True)
        var = jnp.mean(dlt * dlt, axis=-1, keepdims=True)
        vn_ref[:, cols(g)] = (dlt * lax.rsqrt(var + EPS) * lng_ref[:, cols(g)]
                              + lnb_ref[:, cols(g)]).astype(BF16)

    def tile_gate(j):
        sg_ref[:, cols(j)] = _sigmoid(proj_b(COL_GG + j * ct))

    gla = {}

    def gate_low():
        gla["a_low"] = _dot(a_ref[...], wa_ref[:, COL_ALOW:COL_ALOW + LANES]).astype(BF16)

    def gate_logit():
        gla["logit"] = _dot(gla["a_low"], wgu_ref[...]) + bgate_ref[...]

    def gate_scan():
        logit = gla["logit"]
        log_sig = jnp.minimum(logit, 0.0) - jnp.log1p(jnp.exp(-jnp.abs(logit)))
        gla["cum"] = _chunk_cumsum(log_sig * (1.0 / GLA_TAU))

    upd = [[None] * GLA_HEADS for _ in range(n_chunks)]
    dcol = [[None] * GLA_HEADS for _ in range(n_chunks)]
    s_bf = [[None] * GLA_HEADS for _ in range(n_chunks)]

    def gla_update(c):
        r0 = c * CHUNK
        cum_c = gla["cum"][r0:r0 + CHUNK]
        tot_c = gla["cum"][r0 + CHUNK - 1:r0 + CHUNK]
        k_dec = (k_ref[r0:r0 + CHUNK, :] * jnp.exp(tot_c - cum_c)).astype(BF16)
        decay = jnp.exp(tot_c)
        for h in range(GLA_HEADS):
            kd = k_dec[:, h * GLA_DK:(h + 1) * GLA_DK]
            vv = v_ref[r0:r0 + CHUNK, h * GLA_DV:(h + 1) * GLA_DV]
            upd[c][h] = lax.dot_general(kd, vv, (((0,), (0,)), ((), ())),
                                        preferred_element_type=F32)
            dcol[c][h] = jnp.broadcast_to(
                decay[:, h * GLA_DK:(h + 1) * GLA_DK], (GLA_DK, GLA_DK)).T

    def gla_chain():
        for h in range(GLA_HEADS):
            s = state_ref[h]
            for c in range(n_chunks):
                s = jnp.concatenate([dcol[c][h], dcol[c][h]], axis=1) * s + upd[c][h]
                s_bf[c][h] = s.astype(BF16)
            state_ref[h] = s

    def gla_readout(c):
        r0 = c * CHUNK
        for h in range(GLA_HEADS):
            oacc_ref[r0:r0 + CHUNK, h * GLA_DV:(h + 1) * GLA_DV] = _dot(
                q_ref[r0:r0 + CHUNK, h * GLA_DK:(h + 1) * GLA_DK], s_bf[c][h])

    def gla_out(h):
        yg_ref[:, cols(h)] = (_rmsnorm(oacc_ref[:, cols(h)], gnorm_ref[:, cols(h)])
                              * rs_ref[:, cols(h)]).astype(BF16)

    row = lax.broadcasted_iota(jnp.int32, (SGU_BLOCK, SGU_BLOCK), 0)
    col = lax.broadcasted_iota(jnp.int32, (SGU_BLOCK, SGU_BLOCK), 1)
    causal = (col // CHUNK) <= (row // CHUNK)

    def sgu_mix(g):
        w_g = jnp.where(causal, wsp_ref[g], 0.0).astype(BF16)
        for n in range(tm // SGU_BLOCK):
            rs = slice(n * SGU_BLOCK, (n + 1) * SGU_BLOCK)
            mixed = _dot(w_g, vn_ref[rs, cols(g)]) + bsp_ref[:, cols(g)]
            ys_ref[rs, cols(g)] = (u_ref[rs, cols(g)] * mixed).astype(BF16)

    def merge(j):
        bg = _dot(yg_ref[...], wbg_ref[:, cols(j)])
        bs = _dot(ys_ref[...], wbs_ref[:, cols(j)])
        m_ref[:, cols(j)] = (sg_ref[:, cols(j)] * bg
                             + sg_ref[:, D_MODEL + j * ct:D_MODEL + (j + 1) * ct] * bs).astype(BF16)

    assert ct == SGU_DG == GLA_DV and n_chunks % 4 == 0
    gate_low()
    tile_k(0)
    tile_k(1)
    gate_logit()
    tile_q(0)
    tile_q(1)
    gate_scan()
    for j in range(4):
        tile_v(j)
    tile_r(0)
    tile_r(1)
    fill = [lambda: tile_r(2), lambda: tile_r(3), lambda: tile_su(0), lambda: tile_su(1)]
    for i in range(4):
        for c in range(i * n_chunks // 4, (i + 1) * n_chunks // 4):
            gla_update(c)
        fill[i]()
    gla_chain()
    tile_su(2)
    tile_su(3)
    for i in range(4):
        for c in range(i * n_chunks // 4, (i + 1) * n_chunks // 4):
            gla_readout(c)
        tile_sv(i)
    for h in range(GLA_HEADS):
        gla_out(h)
        tile_gate(h)
        sgu_mix(h)
        tile_gate(GLA_HEADS + h)
    for j in range(D_MODEL // ct):
        merge(j)

    ssq = jnp.zeros((tm, 1), F32)
    for j in range(D_MODEL // ct):
        mix = _dot(m_ref[...], wout_ref[:, cols(j)])
        ssq = ssq + jnp.sum(mix * mix, axis=-1, keepdims=True)
        o_ref[0, :, cols(j)] = mix
    scale = lax.rsqrt(ssq * (1.0 / D_MODEL) + EPS)
    o_ref[0] = x_ref[0] + o_ref[0] * scale * gpost_ref[...]


def _ffn_kernel(x_ref, gpre_ref, win_ref, wout_ref, gpost_ref, o_ref, act_ref):
    ct = COL_TILE
    x = x_ref[...]
    h = _rmsnorm(x, gpre_ref[...]).astype(BF16)
    for j in range(D_FF // ct):
        c0 = j * ct
        gate = _dot(h, win_ref[:, c0:c0 + ct])
        up = _dot(h, win_ref[:, D_FF + c0:D_FF + c0 + ct])
        act_ref[:, c0:c0 + ct] = (gate * _sigmoid(gate) * up).astype(BF16)
    y = _dot(act_ref[...], wout_ref[...])
    o_ref[...] = x + _rmsnorm(y, gpost_ref[...])


def _resident(shape):
    zeros = (0,) * len(shape)
    return pl.BlockSpec(shape, lambda *_: zeros, pipeline_mode=pl.Buffered(1))


def _mixer_call(x, gpre, w_a, w_b, w_gu, b_gate, gnorm, lng, lnb, wsp, bsp,
                wbg, wbs, wout, gpost):
    batch, seq, _ = x.shape
    tm = TOKEN_TILE
    consts = (gpre, w_a, w_b, w_gu, b_gate, gnorm, lng, lnb, wsp, bsp,
              wbg, wbs, wout, gpost)
    return pl.pallas_call(
        _mixer_kernel,
        grid=(batch, seq // tm),
        in_specs=[pl.BlockSpec((1, tm, D_MODEL), lambda b, i: (b, i, 0))]
        + [_resident(c.shape) for c in consts],
        out_specs=pl.BlockSpec((1, tm, D_MODEL), lambda b, i: (b, i, 0)),
        out_shape=jax.ShapeDtypeStruct(x.shape, x.dtype),
        scratch_shapes=[
            pltpu.VMEM((GLA_HEADS, GLA_DK, GLA_DV), F32),
            pltpu.VMEM((tm, D_MODEL), BF16),
            pltpu.VMEM((tm, GLA_QK), BF16),
            pltpu.VMEM((tm, GLA_QK), F32),
            pltpu.VMEM((tm, GLA_V), BF16),
            pltpu.VMEM((tm, GLA_V), F32),
            pltpu.VMEM((tm, D_MODEL), F32),
            pltpu.VMEM((tm, D_MODEL), BF16),
            pltpu.VMEM((tm, 2 * D_MODEL), F32),
            pltpu.VMEM((tm, GLA_V), F32),
            pltpu.VMEM((tm, GLA_V), BF16),
            pltpu.VMEM((tm, D_MODEL), BF16),
            pltpu.VMEM((tm, D_MODEL), BF16),
        ],
        compiler_params=pltpu.CompilerParams(
            dimension_semantics=("arbitrary", "arbitrary"),
            vmem_limit_bytes=VMEM_LIMIT_BYTES),
        name="mixer",
    )(x, *consts)


def _ffn_call(x2d, gpre, win, wout, gpost):
    tokens = x2d.shape[0]
    tm = TOKEN_TILE
    consts = (gpre, win, wout, gpost)
    return pl.pallas_call(
        _ffn_kernel,
        grid=(tokens // tm,),
        in_specs=[pl.BlockSpec((tm, D_MODEL), lambda i: (i, 0))]
        + [_resident(c.shape) for c in consts],
        out_specs=pl.BlockSpec((tm, D_MODEL), lambda i: (i, 0)),
        out_shape=jax.ShapeDtypeStruct(x2d.shape, x2d.dtype),
        scratch_shapes=[pltpu.VMEM((tm, D_FF), BF16)],
        compiler_params=pltpu.CompilerParams(
            dimension_semantics=("arbitrary",),
            vmem_limit_bytes=VMEM_LIMIT_BYTES),
        name="ffn",
    )(x2d, *consts)


def kernel(x, norm_pre_mix, w_in, w_gate_up, b_gate, gla_norm, sgu_ln_g, sgu_ln_b,
           w_spatial, b_spatial, w_branch_gla, w_branch_sgu, w_out, norm_post_mix,
           norm_pre_ffn, w_ffn_in, w_ffn_out, norm_post_ffn):
    batch, seq, d = x.shape
    assert d == D_MODEL and seq % TOKEN_TILE == 0
    depth = w_in.shape[0]
    assert w_in.shape[2] == COL_ALOW + GLA_GATE_RANK + WIN_B
    for l in range(depth):
        w_a = w_in[l, :, :WIN_A].astype(BF16)
        w_b = w_in[l, :, COL_ALOW + GLA_GATE_RANK:].astype(BF16)
        w_gu = jnp.pad(w_gate_up[l], ((0, LANES - GLA_GATE_RANK), (0, 0))).astype(BF16)
        bsp = jnp.repeat(b_spatial[l].T, SGU_DG, axis=1)
        x = _mixer_call(
            x, norm_pre_mix[l][None], w_a, w_b, w_gu, b_gate[l][None],
            gla_norm[l].reshape(1, GLA_V), sgu_ln_g[l].reshape(1, D_MODEL),
            sgu_ln_b[l].reshape(1, D_MODEL), w_spatial[l], bsp,
            w_branch_gla[l].astype(BF16), w_branch_sgu[l].astype(BF16),
            w_out[l].astype(BF16), norm_post_mix[l][None])
        x = _ffn_call(
            x.reshape(batch * seq, d), norm_pre_ffn[l][None], w_ffn_in[l].astype(BF16),
            w_ffn_out[l].astype(BF16), norm_post_ffn[l][None]).reshape(batch, seq, d)
    return x
```

```python
import jax
import jax.numpy as jnp
from jax import lax
from jax.experimental import pallas as pl
from jax.experimental.pallas import tpu as pltpu

F32 = jnp.float32
BF16 = jnp.bfloat16

D_MODEL = 1024
CHUNK = 64
GLA_HEADS = 4
GLA_DK = 128
GLA_DV = 256
GLA_QK = GLA_HEADS * GLA_DK
GLA_V = GLA_HEADS * GLA_DV
GLA_GATE_RANK = 16
GLA_TAU = 16.0
SGU_GROUPS = 4
SGU_BLOCK = 128
SGU_DG = 256
D_FF = 2816
EPS = 1e-6

LANES = 128

COL_Q = 0
COL_K = COL_Q + GLA_QK
COL_V = COL_K + GLA_QK
COL_R = COL_V + GLA_V
COL_ALOW = COL_R + GLA_V
WIN_A = COL_ALOW
COL_SU = 0
COL_SV = COL_SU + D_MODEL
COL_GG = COL_SV + D_MODEL
COL_GS = COL_GG + D_MODEL
WIN_B = COL_GS + D_MODEL

TOKEN_TILE = 512
COL_TILE = 256
STAGE_ROWS = 256
FFN_IN_STAGE_ROWS = 64
VMEM_LIMIT_BYTES = 56 * 1024 * 1024


def _dot(a, b):
    return jnp.dot(a, b, preferred_element_type=F32)


def _stream_weights(jobs, stage_ref, sem_ref):
    def copy(i):
        src, row0, rows, _ = jobs[i]
        return pltpu.make_async_copy(
            src.at[pl.ds(row0, rows), :], stage_ref.at[i % 2, pl.ds(0, rows), :],
            sem_ref.at[i % 2])

    copy(0).start()
    for i in range(len(jobs)):
        if i + 1 < len(jobs):
            copy(i + 1).start()
        copy(i).wait()
        jobs[i][3](stage_ref[i % 2, 0:jobs[i][2], :])


def _cast_rows_into(dst_ref, row0):
    def consume(chunk):
        dst_ref[row0:row0 + chunk.shape[0], :] = chunk.astype(BF16)
    return consume


def _rmsnorm(x, g):
    ms = jnp.mean(x * x, axis=-1, keepdims=True)
    return x * lax.rsqrt(ms + EPS) * g


def _sigmoid(x):
    return 1.0 / (1.0 + jnp.exp(-x))


def _gelu_tanh(x):
    c = 0.7978845608028654
    inner = x * (c + (c * 0.044715) * (x * x))
    hx = 0.5 * x
    return hx + hx * jnp.tanh(inner)


def _chunk_cumsum(x):
    row_in_chunk = lax.broadcasted_iota(jnp.int32, x.shape, 0) & (CHUNK - 1)
    shift = 1
    while shift < CHUNK:
        shifted = pltpu.roll(x, shift, axis=0)
        x = x + jnp.where(row_in_chunk >= shift, shifted, 0.0)
        shift *= 2
    return x


def _mixer_kernel(x_ref, gpre_ref, wint_hbm, wgu_ref, bgate_ref, gnorm_ref,
                  lng_ref, lnb_ref, wsp_ref, bsp_ref, wbg_hbm, wbs_hbm, wout_hbm,
                  gpost_ref, o_ref, state_ref, a_ref, q_ref, k_ref, v_ref, rs_ref, u_ref,
                  vn_ref, sg_ref, oacc_ref, yg_ref, ys_ref, m_ref,
                  wa_ref, walow_ref, wb_ref, wbg_ref, wbs_ref, wout_ref, stage_ref, sem_ref):
    tm = x_ref.shape[1]
    n_chunks = tm // CHUNK
    ct = COL_TILE

    @pl.when((pl.program_id(0) == 0) & (pl.program_id(1) == 0))
    def _():
        def cols_into(dst_ref, c0):
            def consume(chunk):
                dst_ref[:, c0:c0 + chunk.shape[0]] = chunk.T.astype(BF16)
            return consume

        def gate_cols(chunk):
            lane = lax.broadcasted_iota(jnp.int32, (D_MODEL, LANES), 1)
            walow_ref[...] = jnp.where(lane < GLA_GATE_RANK, chunk.T, 0.0).astype(BF16)

        jobs = [(wint_hbm, c0, STAGE_ROWS, cols_into(wa_ref, c0))
                for c0 in range(0, WIN_A, STAGE_ROWS)]
        jobs.append((wint_hbm, COL_ALOW, LANES, gate_cols))
        jobs += [(wint_hbm, COL_ALOW + GLA_GATE_RANK + c0, STAGE_ROWS, cols_into(wb_ref, c0))
                 for c0 in range(0, WIN_B, STAGE_ROWS)]
        for src, dst in ((wbg_hbm, wbg_ref), (wbs_hbm, wbs_ref), (wout_hbm, wout_ref)):
            jobs += [(src, r0, STAGE_ROWS, _cast_rows_into(dst, r0))
                     for r0 in range(0, D_MODEL, STAGE_ROWS)]
        _stream_weights(jobs, stage_ref, sem_ref)

    @pl.when(pl.program_id(1) == 0)
    def _():
        state_ref[...] = jnp.zeros_like(state_ref)

    a_ref[...] = _rmsnorm(x_ref[0], gpre_ref[...]).astype(BF16)

    def proj_a(c0):
        return _dot(a_ref[...], wa_ref[:, c0:c0 + ct])

    def proj_b(c0):
        return _dot(a_ref[...], wb_ref[:, c0:c0 + ct])

    def cols(j):
        return slice(j * ct, (j + 1) * ct)

    def tile_k(j):
        k_ref[:, cols(j)] = proj_a(COL_K + j * ct)

    def tile_q(j):
        q_ref[:, cols(j)] = (proj_a(COL_Q + j * ct) * (GLA_DK ** -0.5)).astype(BF16)

    def tile_v(j):
        v_ref[:, cols(j)] = proj_a(COL_V + j * ct).astype(BF16)

    def tile_r(j):
        r = proj_a(COL_R + j * ct)
        rs_ref[:, cols(j)] = r * _sigmoid(r)

    def tile_su(j):
        u_ref[:, cols(j)] = _gelu_tanh(proj_b(COL_SU + j * ct))

    def tile_sv(g):
        vg = _gelu_tanh(proj_b(COL_SV + g * ct))
        dlt = vg - jnp.mean(vg, axis=-1, keepdims=True)
        var = jnp.mean(dlt * dlt, axis=-1, keepdims=True)
        vn_ref[:, cols(g)] = (dlt * lax.rsqrt(var + EPS) * lng_ref[g:g + 1, :]
                              + lnb_ref[g:g + 1, :]).astype(BF16)

    def tile_gate(j):
        sg_ref[:, cols(j)] = _sigmoid(proj_b(COL_GG + j * ct))

    gla = {}

    def gate_low():
        gla["a_low"] = _dot(a_ref[...], walow_ref[...]).astype(BF16)

    def gate_logit():
        w_gu = jnp.concatenate(
            [wgu_ref[...].astype(BF16), jnp.zeros((LANES - GLA_GATE_RANK, GLA_QK), BF16)], axis=0)
        gla["logit"] = _dot(gla["a_low"], w_gu) + bgate_ref[...]

    def gate_scan():
        logit = gla["logit"]
        log_sig = jnp.minimum(logit, 0.0) - jnp.log1p(jnp.exp(-jnp.abs(logit)))
        gla["cum"] = _chunk_cumsum(log_sig * (1.0 / GLA_TAU))

    upd = [[None] * GLA_HEADS for _ in range(n_chunks)]
    dcol = [[None] * GLA_HEADS for _ in range(n_chunks)]
    s_bf = [[None] * GLA_HEADS for _ in range(n_chunks)]

    def gla_update(c):
        r0 = c * CHUNK
        cum_c = gla["cum"][r0:r0 + CHUNK]
        tot_c = gla["cum"][r0 + CHUNK - 1:r0 + CHUNK]
        k_dec = (k_ref[r0:r0 + CHUNK, :] * jnp.exp(tot_c - cum_c)).astype(BF16)
        decay = jnp.exp(tot_c)
        for h in range(GLA_HEADS):
            kd = k_dec[:, h * GLA_DK:(h + 1) * GLA_DK]
            vv = v_ref[r0:r0 + CHUNK, h * GLA_DV:(h + 1) * GLA_DV]
            upd[c][h] = lax.dot_general(kd, vv, (((0,), (0,)), ((), ())),
                                        preferred_element_type=F32)
            dcol[c][h] = jnp.broadcast_to(
                decay[:, h * GLA_DK:(h + 1) * GLA_DK], (GLA_DK, GLA_DK)).T

    def gla_chain():
        for h in range(GLA_HEADS):
            s = state_ref[h]
            for c in range(n_chunks):
                s = jnp.concatenate([dcol[c][h], dcol[c][h]], axis=1) * s + upd[c][h]
                s_bf[c][h] = s.astype(BF16)
            state_ref[h] = s

    def gla_readout(c):
        r0 = c * CHUNK
        for h in range(GLA_HEADS):
            oacc_ref[r0:r0 + CHUNK, h * GLA_DV:(h + 1) * GLA_DV] = _dot(
                q_ref[r0:r0 + CHUNK, h * GLA_DK:(h + 1) * GLA_DK], s_bf[c][h])

    def gla_out(h):
        yg_ref[:, cols(h)] = (_rmsnorm(oacc_ref[:, cols(h)], gnorm_ref[h:h + 1, :])
                              * rs_ref[:, cols(h)]).astype(BF16)

    row = lax.broadcasted_iota(jnp.int32, (SGU_BLOCK, SGU_BLOCK), 0)
    col = lax.broadcasted_iota(jnp.int32, (SGU_BLOCK, SGU_BLOCK), 1)
    causal = (col // CHUNK) <= (row // CHUNK)

    def sgu_mix(g):
        w_g = jnp.where(causal, wsp_ref[g], 0.0).astype(BF16)
        b_pos = jnp.broadcast_to(bsp_ref[g:g + 1, :], (SGU_BLOCK, SGU_BLOCK)).T
        bias = jnp.concatenate([b_pos] * (SGU_DG // SGU_BLOCK), axis=1)
        for n in range(tm // SGU_BLOCK):
            rs = slice(n * SGU_BLOCK, (n + 1) * SGU_BLOCK)
            mixed = _dot(w_g, vn_ref[rs, cols(g)]) + bias
            ys_ref[rs, cols(g)] = (u_ref[rs, cols(g)] * mixed).astype(BF16)

    def merge(j):
        bg = _dot(yg_ref[...], wbg_ref[:, cols(j)])
        bs = _dot(ys_ref[...], wbs_ref[:, cols(j)])
        m_ref[:, cols(j)] = (sg_ref[:, cols(j)] * bg
                             + sg_ref[:, D_MODEL + j * ct:D_MODEL + (j + 1) * ct] * bs).astype(BF16)

    assert ct == SGU_DG == GLA_DV and n_chunks % 4 == 0
    gate_low()
    tile_k(0)
    tile_k(1)
    gate_logit()
    tile_q(0)
    tile_q(1)
    gate_scan()
    for j in range(4):
        tile_v(j)
    tile_r(0)
    tile_r(1)
    fill = [lambda: tile_r(2), lambda: tile_r(3), lambda: tile_su(0), lambda: tile_su(1)]
    for i in range(4):
        for c in range(i * n_chunks // 4, (i + 1) * n_chunks // 4):
            gla_update(c)
        fill[i]()
    gla_chain()
    tile_su(2)
    tile_su(3)
    for i in range(4):
        for c in range(i * n_chunks // 4, (i + 1) * n_chunks // 4):
            gla_readout(c)
        tile_sv(i)
    for h in range(GLA_HEADS):
        gla_out(h)
        tile_gate(h)
        sgu_mix(h)
        tile_gate(GLA_HEADS + h)
    for j in range(D_MODEL // ct):
        merge(j)

    ssq = jnp.zeros((tm, 1), F32)
    for j in range(D_MODEL // ct):
        mix = _dot(m_ref[...], wout_ref[:, cols(j)])
        ssq = ssq + jnp.sum(mix * mix, axis=-1, keepdims=True)
        o_ref[0, :, cols(j)] = mix
    scale = lax.rsqrt(ssq * (1.0 / D_MODEL) + EPS)
    o_ref[0] = x_ref[0] + o_ref[0] * scale * gpost_ref[...]


def _ffn_kernel(x_ref, gpre_ref, win_hbm, wout_hbm, gpost_ref, o_ref, act_ref,
                win_ref, wout_ref, stage_in_ref, stage_out_ref, sem_ref):
    ct = COL_TILE

    @pl.when(pl.program_id(0) == 0)
    def _():
        _stream_weights(
            [(win_hbm, r0, FFN_IN_STAGE_ROWS, _cast_rows_into(win_ref, r0))
             for r0 in range(0, D_MODEL, FFN_IN_STAGE_ROWS)], stage_in_ref, sem_ref)
        _stream_weights(
            [(wout_hbm, r0, STAGE_ROWS, _cast_rows_into(wout_ref, r0))
             for r0 in range(0, D_FF, STAGE_ROWS)], stage_out_ref, sem_ref)

    x = x_ref[...]
    h = _rmsnorm(x, gpre_ref[...]).astype(BF16)
    for j in range(D_FF // ct):
        c0 = j * ct
        gate = _dot(h, win_ref[:, c0:c0 + ct])
        up = _dot(h, win_ref[:, D_FF + c0:D_FF + c0 + ct])
        act_ref[:, c0:c0 + ct] = (gate * _sigmoid(gate) * up).astype(BF16)
    y = _dot(act_ref[...], wout_ref[...])
    o_ref[...] = x + _rmsnorm(y, gpost_ref[...])


def _resident(shape):
    zeros = (0,) * len(shape)
    return pl.BlockSpec(shape, lambda *_: zeros, pipeline_mode=pl.Buffered(1))


def _mixer_call(x, gpre, w_in_t, w_gu, b_gate, gnorm, lng, lnb, wsp, bsp,
                wbg, wbs, wout, gpost):
    batch, seq, _ = x.shape
    tm = TOKEN_TILE
    hbm = pl.BlockSpec(memory_space=pl.ANY)
    args = (gpre, w_in_t, w_gu, b_gate, gnorm, lng, lnb, wsp, bsp, wbg, wbs, wout, gpost)
    in_hbm = (w_in_t, wbg, wbs, wout)
    return pl.pallas_call(
        _mixer_kernel,
        grid=(batch, seq // tm),
        in_specs=[pl.BlockSpec((1, tm, D_MODEL), lambda b, i: (b, i, 0))]
        + [hbm if any(c is h for h in in_hbm) else _resident(c.shape) for c in args],
        out_specs=pl.BlockSpec((1, tm, D_MODEL), lambda b, i: (b, i, 0)),
        out_shape=jax.ShapeDtypeStruct(x.shape, x.dtype),
        scratch_shapes=[
            pltpu.VMEM((GLA_HEADS, GLA_DK, GLA_DV), F32),
            pltpu.VMEM((tm, D_MODEL), BF16),
            pltpu.VMEM((tm, GLA_QK), BF16),
            pltpu.VMEM((tm, GLA_QK), F32),
            pltpu.VMEM((tm, GLA_V), BF16),
            pltpu.VMEM((tm, GLA_V), F32),
            pltpu.VMEM((tm, D_MODEL), F32),
            pltpu.VMEM((tm, D_MODEL), BF16),
            pltpu.VMEM((tm, 2 * D_MODEL), F32),
            pltpu.VMEM((tm, GLA_V), F32),
            pltpu.VMEM((tm, GLA_V), BF16),
            pltpu.VMEM((tm, D_MODEL), BF16),
            pltpu.VMEM((tm, D_MODEL), BF16),
            pltpu.VMEM((D_MODEL, WIN_A), BF16),
            pltpu.VMEM((D_MODEL, LANES), BF16),
            pltpu.VMEM((D_MODEL, WIN_B), BF16),
            pltpu.VMEM((GLA_V, D_MODEL), BF16),
            pltpu.VMEM((D_MODEL, D_MODEL), BF16),
            pltpu.VMEM((D_MODEL, D_MODEL), BF16),
            pltpu.VMEM((2, STAGE_ROWS, D_MODEL), F32),
            pltpu.SemaphoreType.DMA((2,)),
        ],
        compiler_params=pltpu.CompilerParams(
            dimension_semantics=("arbitrary", "arbitrary"),
            vmem_limit_bytes=VMEM_LIMIT_BYTES),
        name="mixer",
    )(x, *args)


def _ffn_call(x2d, gpre, win, wout, gpost):
    tokens = x2d.shape[0]
    tm = TOKEN_TILE
    hbm = pl.BlockSpec(memory_space=pl.ANY)
    return pl.pallas_call(
        _ffn_kernel,
        grid=(tokens // tm,),
        in_specs=[pl.BlockSpec((tm, D_MODEL), lambda i: (i, 0)),
                  _resident(gpre.shape), hbm, hbm, _resident(gpost.shape)],
        out_specs=pl.BlockSpec((tm, D_MODEL), lambda i: (i, 0)),
        out_shape=jax.ShapeDtypeStruct(x2d.shape, x2d.dtype),
        scratch_shapes=[
            pltpu.VMEM((tm, D_FF), BF16),
            pltpu.VMEM((D_MODEL, 2 * D_FF), BF16),
            pltpu.VMEM((D_FF, D_MODEL), BF16),
            pltpu.VMEM((2, FFN_IN_STAGE_ROWS, 2 * D_FF), F32),
            pltpu.VMEM((2, STAGE_ROWS, D_MODEL), F32),
            pltpu.SemaphoreType.DMA((2,)),
        ],
        compiler_params=pltpu.CompilerParams(
            dimension_semantics=("arbitrary",),
            vmem_limit_bytes=VMEM_LIMIT_BYTES),
        name="ffn",
    )(x2d, gpre, win, wout, gpost)


def kernel(x, norm_pre_mix, w_in, w_gate_up, b_gate, gla_norm, sgu_ln_g, sgu_ln_b,
           w_spatial, b_spatial, w_branch_gla, w_branch_sgu, w_out, norm_post_mix,
           norm_pre_ffn, w_ffn_in, w_ffn_out, norm_post_ffn):
    batch, seq, d = x.shape
    assert d == D_MODEL and seq % TOKEN_TILE == 0
    depth = w_in.shape[0]
    assert w_in.shape[2] == COL_ALOW + GLA_GATE_RANK + WIN_B
    for l in range(depth):
        x = _mixer_call(
            x, norm_pre_mix[l][None], w_in[l].T, w_gate_up[l], b_gate[l][None],
            gla_norm[l], sgu_ln_g[l], sgu_ln_b[l], w_spatial[l], b_spatial[l],
            w_branch_gla[l], w_branch_sgu[l], w_out[l], norm_post_mix[l][None])
        x = _ffn_call(
            x.reshape(batch * seq, d), norm_pre_ffn[l][None], w_ffn_in[l],
            w_ffn_out[l], norm_post_ffn[l][None]).reshape(batch, seq, d)
    return x
```
